```python
import math
import jax, jax.numpy as jnp
from jax import lax
import numpy as np

D_MODEL = 1024
BATCH = 32
SEQ = 2048
DEPTH = 4

N_BRANCH = 4
BRANCH_WIDTH = 512
SSD_D_INNER = 512
SSD_HEADDIM = 64
SSD_HEADS = SSD_D_INNER // SSD_HEADDIM
SSD_GROUPS = 2
SSD_STATE = 128
SSD_CONV = 4
SSD_CHUNK = 128
SSD_XBC = SSD_D_INNER + 2 * SSD_GROUPS * SSD_STATE
GMLP_WIDTH = 512
GMLP_GROUPS = 4
GMLP_CHUNK = 128
MLA_HEADS = 4
MLA_Q_RANK = 384
MLA_KV_RANK = 128
MLA_NOPE = 128
MLA_ROPE = 64
MLA_V = 128
MLA_QK = MLA_NOPE + MLA_ROPE
ROPE_THETA = 10000.0
ATTN_BLOCK = 128
SC_WIDTH = 512
SC_KERNEL = 3
D_FF = 2816
NORM_EPS = 1e-6
IN_SIZES = (SSD_D_INNER, SSD_XBC, SSD_HEADS, 2 * GMLP_WIDTH, MLA_Q_RANK, MLA_KV_RANK, MLA_ROPE, 3 * SC_WIDTH, N_BRANCH * D_MODEL)
IN_TOTAL = sum(IN_SIZES)

kernel_name = "hybrid_gated_parallel_mixer_block"

F32 = jnp.float32


def rms_norm(x, w):
    x32 = x.astype(F32)
    y = x32 * lax.rsqrt(jnp.mean(x32 * x32, axis=-1, keepdims=True) + NORM_EPS)
    return (y * w.astype(F32)).astype(x.dtype)


def split_cols(x, sizes):
    out = []
    off = 0
    for s in sizes:
        out.append(x[..., off:off + s])
        off += s
    return out


def causal_dwconv(x, w):
    K = w.shape[0]
    L = x.shape[1]
    xp = jnp.pad(x, ((0, 0), (K - 1, 0), (0, 0)))
    out = xp[:, 0:L] * w[0]
    for k in range(1, K):
        out = out + xp[:, k:k + L] * w[k]
    return out


def swiglu(h, w_gu, w_down):
    g, u = jnp.split(h @ w_gu, 2, axis=-1)
    return (jax.nn.silu(g) * u) @ w_down


def apply_rope(x, cos, sin):
    x1, x2 = jnp.split(x.astype(F32), 2, axis=-1)
    out = jnp.concatenate([x1 * cos - x2 * sin, x2 * cos + x1 * sin], axis=-1)
    return out.astype(x.dtype)


def ssd_chunked_scan(xh, dt, A, Bg, Cg):
    b, L, H, P = xh.shape
    G, N = Bg.shape[2], Bg.shape[3]
    R = H // G
    Q = SSD_CHUNK
    c = L // Q
    dtc = dt.reshape(b, c, Q, G, R)
    a_cs = jnp.cumsum(dtc * A.reshape(G, R), axis=2)
    xdt = xh.astype(F32).reshape(b, c, Q, G, R, P) * dtc[..., None]
    Bc = Bg.astype(F32).reshape(b, c, Q, G, N)
    Cc = Cg.astype(F32).reshape(b, c, Q, G, N)
    causal = jnp.tril(jnp.ones((Q, Q), bool))[None, None, :, :, None, None]
    seg = a_cs[:, :, :, None] - a_cs[:, :, None, :]
    decay_in = jnp.exp(jnp.where(causal, seg, -jnp.inf))
    cb = jnp.einsum('bctgn,bcsgn->bctsg', Cc, Bc)
    y_diag = jnp.einsum('bctsgr,bcsgrp->bctgrp', cb[..., None] * decay_in, xdt)
    decay_out = jnp.exp(a_cs[:, :, -1:] - a_cs)
    states = jnp.einsum('bcsgn,bcsgrp->bcgrpn', Bc, xdt * decay_out[..., None])
    a_tot = a_cs[:, :, -1]

    def step(h, inp):
        a_c, s_c = inp
        return jnp.exp(a_c)[..., None, None] * h + s_c, h

    h0 = jnp.zeros((b, G, R, P, N), F32)
    _, prev = lax.scan(step, h0, (jnp.moveaxis(a_tot, 1, 0), jnp.moveaxis(states, 1, 0)))
    prev = jnp.moveaxis(prev, 0, 1)
    y_off = jnp.einsum('bctgn,bcgrpn->bctgrp', Cc, prev) * jnp.exp(a_cs)[..., None]
    return (y_diag + y_off).reshape(b, L, H, P).astype(xh.dtype)


def ssd_branch(z, xbc_raw, dt_raw, conv_w, conv_b, dt_bias, a_log, d_skip, norm_w):
    b, L, _ = z.shape
    xbc = jax.nn.silu(causal_dwconv(xbc_raw, conv_w) + conv_b)
    xs, Bm, Cm = split_cols(xbc, (SSD_D_INNER, SSD_GROUPS * SSD_STATE, SSD_GROUPS * SSD_STATE))
    dt = jax.nn.softplus(dt_raw.astype(F32) + dt_bias.astype(F32))
    A = -jnp.exp(a_log.astype(F32))
    xh = xs.reshape(b, L, SSD_HEADS, SSD_HEADDIM)
    y = ssd_chunked_scan(xh, dt, A,
                         Bm.reshape(b, L, SSD_GROUPS, SSD_STATE),
                         Cm.reshape(b, L, SSD_GROUPS, SSD_STATE))
    y = y + d_skip[:, None] * xh
    y = y.reshape(b, L, SSD_D_INNER) * jax.nn.silu(z)
    y = rms_norm(y.reshape(b, L, SSD_GROUPS, SSD_D_INNER // SSD_GROUPS),
                 norm_w.reshape(SSD_GROUPS, SSD_D_INNER // SSD_GROUPS))
    return y.reshape(b, L, SSD_D_INNER)


def gmlp_branch(uv_raw, v_norm, w_s, b_s):
    b, L, _ = uv_raw.shape
    Q = GMLP_CHUNK
    c = L // Q
    dg = GMLP_WIDTH // GMLP_GROUPS
    u, v = jnp.split(jax.nn.gelu(uv_raw, approximate=False), 2, axis=-1)
    v = rms_norm(v, v_norm).reshape(b, c, Q, GMLP_GROUPS, dg)
    w_causal = w_s * jnp.tril(jnp.ones((Q, Q), w_s.dtype))
    sv = jnp.einsum('gts,bcsgd->bctgd', w_causal, v) + b_s.T[:, :, None]
    return u * sv.reshape(b, L, GMLP_WIDTH)


def causal_block_attention(q, k, v, scale):
    b, L, H, dq = q.shape
    nblk = L // ATTN_BLOCK
    qb = jnp.swapaxes(q.reshape(b, nblk, ATTN_BLOCK, H, dq), 0, 1)
    key_pos = jnp.arange(L)

    def one_block(args):
        q_blk, i = args
        s = jnp.einsum('bqhd,bkhd->bhqk', q_blk, k).astype(F32) * scale
        q_pos = i * ATTN_BLOCK + jnp.arange(ATTN_BLOCK)
        mask = key_pos[None, :] <= q_pos[:, None]
        p = jax.nn.softmax(jnp.where(mask, s, -jnp.inf), axis=-1)
        return jnp.einsum('bhqk,bkhd->bqhd', p.astype(v.dtype), v)

    out = lax.map(one_block, (qb, jnp.arange(nblk)))
    return jnp.swapaxes(out, 0, 1).reshape(b, L, H, v.shape[-1])


def mla_branch(q_lat, kv_lat, k_pe, cos, sin, q_norm, w_qb, kv_norm, w_kvb, qk_q, qk_k):
    b, L, _ = q_lat.shape
    H = MLA_HEADS
    q = (rms_norm(q_lat, q_norm) @ w_qb).reshape(b, L, H, MLA_QK)
    kv = (rms_norm(kv_lat, kv_norm) @ w_kvb).reshape(b, L, H, MLA_NOPE + MLA_V)
    k_nope, v = kv[..., :MLA_NOPE], kv[..., MLA_NOPE:]
    k = jnp.concatenate([k_nope, jnp.broadcast_to(k_pe[:, :, None, :], (b, L, H, MLA_ROPE))], axis=-1)
    q = rms_norm(q, qk_q)
    k = rms_norm(k, qk_k)
    q = jnp.concatenate([q[..., :MLA_NOPE], apply_rope(q[..., MLA_NOPE:], cos, sin)], axis=-1)
    k = jnp.concatenate([k[..., :MLA_NOPE], apply_rope(k[..., MLA_NOPE:], cos, sin)], axis=-1)
    o = causal_block_attention(q, k, v, MLA_QK ** -0.5)
    return o.reshape(b, L, H * MLA_V)


def short_conv_branch(sc_raw, conv_w):
    bg, cg, xin = jnp.split(sc_raw, 3, axis=-1)
    return bg * causal_dwconv(cg * xin, conv_w)


def hybrid_mixer(h, cos, sin, w_in, ssd_conv_w, ssd_conv_b, ssd_dt_bias, ssd_a_log, ssd_d, ssd_norm,
                 gmlp_v_norm, gmlp_w_s, gmlp_b_s, mla_q_norm, mla_w_qb, mla_kv_norm, mla_w_kvb,
                 mla_qk_q, mla_qk_k, sc_conv_w, w_branch, w_out):
    b, L, _ = h.shape
    proj = h @ w_in
    z, xbc, dt_raw, uv, q_lat, kv_lat, k_pe, sc, gates = split_cols(proj, IN_SIZES)
    y_a = ssd_branch(z, xbc, dt_raw, ssd_conv_w, ssd_conv_b, ssd_dt_bias, ssd_a_log, ssd_d, ssd_norm)
    y_b = gmlp_branch(uv, gmlp_v_norm, gmlp_w_s, gmlp_b_s)
    y_c = mla_branch(q_lat, kv_lat, k_pe, cos, sin, mla_q_norm, mla_w_qb, mla_kv_norm, mla_w_kvb, mla_qk_q, mla_qk_k)
    y_d = short_conv_branch(sc, sc_conv_w)
    br = jnp.stack([y_a, y_b, y_c, y_d], axis=2)
    per = jnp.einsum('blnd,nde->blne', br, w_branch)
    gate = jax.nn.sigmoid(gates.reshape(b, L, N_BRANCH, D_MODEL))
    return jnp.sum(gate * per, axis=2) @ w_out


def setup_inputs(seed: int = 0) -> dict:
    key = jax.random.key(seed)
    ks = jax.random.split(key, 32)

    def nrm(k, shape, scale):
        return jax.random.normal(k, shape, F32) * scale

    def gain(k, shape):
        return 1.0 + 0.02 * jax.random.normal(k, shape, F32)

    dt0 = jnp.exp(jax.random.uniform(ks[9], (DEPTH, SSD_HEADS), F32, math.log(1e-3), math.log(1e-1)))
    return {
        "x": nrm(ks[0], (BATCH, SEQ, D_MODEL), 1.0),
        "positions": jnp.arange(SEQ, dtype=jnp.int32)[None, :] + jax.random.randint(ks[1], (BATCH, 1), 0, SEQ, dtype=jnp.int32),
        "ffn1_norm": gain(ks[2], (DEPTH, D_MODEL)),
        "ffn1_w_gu": nrm(ks[3], (DEPTH, D_MODEL, 2 * D_FF), D_MODEL ** -0.5),
        "ffn1_w_down": nrm(ks[4], (DEPTH, D_FF, D_MODEL), D_FF ** -0.5),
        "mix_norm": gain(ks[5], (DEPTH, D_MODEL)),
        "w_in": nrm(ks[6], (DEPTH, D_MODEL, IN_TOTAL), D_MODEL ** -0.5),
        "ssd_conv_w": nrm(ks[7], (DEPTH, SSD_CONV, SSD_XBC), SSD_CONV ** -0.5),
        "ssd_conv_b": nrm(ks[8], (DEPTH, SSD_XBC), 0.02),
        "ssd_dt_bias": dt0 + jnp.log(-jnp.expm1(-dt0)),
        "ssd_a_log": jnp.log(jax.random.uniform(ks[10], (DEPTH, SSD_HEADS), F32, 1.0, 16.0)),
        "ssd_d": 1.0 + 0.1 * jax.random.normal(ks[11], (DEPTH, SSD_HEADS), F32),
        "ssd_norm": gain(ks[12], (DEPTH, SSD_D_INNER)),
        "gmlp_v_norm": gain(ks[13], (DEPTH, GMLP_WIDTH)),
        "gmlp_w_s": nrm(ks[14], (DEPTH, GMLP_GROUPS, GMLP_CHUNK, GMLP_CHUNK), GMLP_CHUNK ** -0.5),
        "gmlp_b_s": 1.0 + 0.02 * jax.random.normal(ks[15], (DEPTH, GMLP_GROUPS, GMLP_CHUNK), F32),
        "mla_q_norm": gain(ks[16], (DEPTH, MLA_Q_RANK)),
        "mla_w_qb": nrm(ks[17], (DEPTH, MLA_Q_RANK, MLA_HEADS * MLA_QK), MLA_Q_RANK ** -0.5),
        "mla_kv_norm": gain(ks[18], (DEPTH, MLA_KV_RANK)),
        "mla_w_kvb": nrm(ks[19], (DEPTH, MLA_KV_RANK, MLA_HEADS * (MLA_NOPE + MLA_V)), MLA_KV_RANK ** -0.5),
        "mla_qk_q": gain(ks[20], (DEPTH, MLA_QK)),
        "mla_qk_k": gain(ks[21], (DEPTH, MLA_QK)),
        "sc_conv_w": nrm(ks[22], (DEPTH, SC_KERNEL, SC_WIDTH), SC_KERNEL ** -0.5),
        "w_branch": nrm(ks[23], (DEPTH, N_BRANCH, BRANCH_WIDTH, D_MODEL), BRANCH_WIDTH ** -0.5),
        "w_out": nrm(ks[24], (DEPTH, D_MODEL, D_MODEL), D_MODEL ** -0.5),
        "ffn2_norm": gain(ks[25], (DEPTH, D_MODEL)),
        "ffn2_w_gu": nrm(ks[26], (DEPTH, D_MODEL, 2 * D_FF), D_MODEL ** -0.5),
        "ffn2_w_down": nrm(ks[27], (DEPTH, D_FF, D_MODEL), D_FF ** -0.5),
    }


def reference(x, positions, ffn1_norm, ffn1_w_gu, ffn1_w_down, mix_norm, w_in, ssd_conv_w, ssd_conv_b,
              ssd_dt_bias, ssd_a_log, ssd_d, ssd_norm, gmlp_v_norm, gmlp_w_s, gmlp_b_s, mla_q_norm,
              mla_w_qb, mla_kv_norm, mla_w_kvb, mla_qk_q, mla_qk_k, sc_conv_w, w_branch, w_out,
              ffn2_norm, ffn2_w_gu, ffn2_w_down):
    inv_freq = ROPE_THETA ** (-jnp.arange(0, MLA_ROPE, 2, dtype=F32) / MLA_ROPE)
    ang = positions.astype(F32)[..., None] * inv_freq
    cos = jnp.cos(ang)[:, :, None, :]
    sin = jnp.sin(ang)[:, :, None, :]
    for l in range(DEPTH):
        x = x + 0.5 * swiglu(rms_norm(x, ffn1_norm[l]), ffn1_w_gu[l], ffn1_w_down[l])
        x = x + hybrid_mixer(rms_norm(x, mix_norm[l]), cos, sin, w_in[l], ssd_conv_w[l], ssd_conv_b[l],
                             ssd_dt_bias[l], ssd_a_log[l], ssd_d[l], ssd_norm[l], gmlp_v_norm[l],
                             gmlp_w_s[l], gmlp_b_s[l], mla_q_norm[l], mla_w_qb[l], mla_kv_norm[l],
                             mla_w_kvb[l], mla_qk_q[l], mla_qk_k[l], sc_conv_w[l], w_branch[l], w_out[l])
        x = x + 0.5 * swiglu(rms_norm(x, ffn2_norm[l]), ffn2_w_gu[l], ffn2_w_down[l])
    return x
```

```python
import functools
import math

import jax
import jax.numpy as jnp
from jax import lax
from jax.experimental import pallas as pl
from jax.experimental.pallas import tpu as pltpu

F32 = jnp.float32
BF16 = jnp.bfloat16

D_MODEL = 1024
D_FF = 2816
NORM_EPS = 1e-6
N_BRANCH = 4
BRANCH_WIDTH = 512
SSD_D_INNER = 512
SSD_HEADDIM = 64
SSD_HEADS = 8
SSD_GROUPS = 2
SSD_HEADS_PER_GROUP = SSD_HEADS // SSD_GROUPS
SSD_GROUP_WIDTH = SSD_D_INNER // SSD_GROUPS
SSD_STATE = 128
SSD_CONV = 4
SSD_XBC = 1024
CHUNK = 128
GMLP_WIDTH = 512
GMLP_GROUPS = 4
GMLP_GROUP_WIDTH = GMLP_WIDTH // GMLP_GROUPS
MLA_HEADS = 4
MLA_Q_RANK = 384
MLA_KV_RANK = 128
MLA_NOPE = 128
MLA_ROPE = 64
MLA_V = 128
MLA_QK = MLA_NOPE + MLA_ROPE
MLA_QK_PAD = 256
ROPE_THETA = 10000.0
SC_WIDTH = 512
SC_KERNEL = 3

LANES = 128
CARRY_ROWS = 8
VMEM_LIMIT_BYTES = 56 * 1024 * 1024

C_Z = 0
C_XBC = C_Z + SSD_D_INNER
C_DT = C_XBC + SSD_XBC
C_UV = C_DT + LANES
C_Q = C_UV + 2 * GMLP_WIDTH
C_KV = C_Q + MLA_Q_RANK
C_KPE = C_KV + MLA_KV_RANK
C_SC = C_KPE + LANES
C_END = C_SC + 3 * SC_WIDTH

FFN_TM = 512
MERGE_TM = 512
BRANCH_TQ = 512


def _dot(a, b):
    return jnp.dot(a, b, preferred_element_type=F32)


def _dot_nt(a, b):
    return lax.dot_general(a, b, (((1,), (1,)), ((), ())), preferred_element_type=F32)


def _dot_tn(a, b):
    return lax.dot_general(a, b, (((0,), (0,)), ((), ())), preferred_element_type=F32)


def _split_bf16(x):
    hi = x.astype(BF16)
    lo = (x - hi.astype(F32)).astype(BF16)
    return hi, lo


def _rms(x, w, width=None):
    n = x.shape[-1] if width is None else width
    ms = jnp.sum(x * x, axis=-1, keepdims=True) * (1.0 / n)
    return x * lax.rsqrt(ms + NORM_EPS) * w


def _silu(x):
    return x * jax.nn.sigmoid(x)


def _const_spec(shape):
    nd = len(shape)
    return pl.BlockSpec(shape, lambda *_: (0,) * nd, pipeline_mode=pl.Buffered(1))


def _rope_table_body(pos_ref, freq_ref, cos_ref, sin_ref):
    ang = pos_ref[...].astype(F32) * freq_ref[...]
    cos_ref[...] = jnp.cos(ang)
    sin_ref[...] = jnp.sin(ang)


def _rope_tables(positions):
    b, l = positions.shape
    half = MLA_ROPE // 2
    inv_freq = ROPE_THETA ** (-jnp.arange(0, MLA_ROPE, 2, dtype=F32) / MLA_ROPE)
    per_row = LANES // half
    rows = b * l // per_row
    pos = jnp.repeat(positions.reshape(rows, per_row), half, axis=1)
    freq = jnp.tile(inv_freq, per_row).reshape(1, LANES)
    tr = min(rows, 2048)
    cos, sin = pl.pallas_call(
        _rope_table_body,
        out_shape=(jax.ShapeDtypeStruct((rows, LANES), F32),) * 2,
        grid=(rows // tr,),
        in_specs=[pl.BlockSpec((tr, LANES), lambda i: (i, 0)), pl.BlockSpec((1, LANES), lambda i: (0, 0))],
        out_specs=(pl.BlockSpec((tr, LANES), lambda i: (i, 0)),) * 2,
        name="rope_tables",
    )(pos, freq)
    cos = cos.reshape(b, l, half)
    sin = sin.reshape(b, l, half)
    zeros = jnp.zeros((b, l, LANES - 2 * half), F32)
    cosf = jnp.concatenate([cos, cos, zeros], axis=-1)
    sinf = jnp.concatenate([-sin, sin, zeros], axis=-1)
    return cosf, sinf


def _ffn_body(x_ref, nw_ref, wgu_ref, wd_ref, o_ref, act_ref):
    x = x_ref[...]
    h = _rms(x, nw_ref[...]).astype(BF16)
    half = D_FF // 2
    for lo in (0, half):
        g = _dot(h, wgu_ref[:, lo:lo + half])
        u = _dot(h, wgu_ref[:, D_FF + lo:D_FF + lo + half])
        act_ref[:, lo:lo + half] = (_silu(g) * u).astype(BF16)
    o_ref[...] = x + 0.5 * _dot(act_ref[...], wd_ref[...])


def _ffn_call(x, nw, wgu, wd, tm):
    t = x.shape[0]
    return pl.pallas_call(
        _ffn_body,
        out_shape=jax.ShapeDtypeStruct((t, D_MODEL), F32),
        grid=(t // tm,),
        in_specs=[
            pl.BlockSpec((tm, D_MODEL), lambda i: (i, 0)),
            _const_spec((1, D_MODEL)),
            _const_spec((D_MODEL, 2 * D_FF)),
            _const_spec((D_FF, D_MODEL)),
        ],
        out_specs=pl.BlockSpec((tm, D_MODEL), lambda i: (i, 0)),
        scratch_shapes=[pltpu.VMEM((tm, D_FF), BF16)],
        compiler_params=pltpu.CompilerParams(dimension_semantics=("arbitrary",), vmem_limit_bytes=VMEM_LIMIT_BYTES),
        name="ffn",
    )(x, nw, wgu, wd)


def _merge_body(x_ref, y_ref, nw_ref, wg_ref, wbr_ref, wout_ref, o_ref):
    x = x_ref[...]
    h = _rms(x, nw_ref[...]).astype(BF16)
    acc = None
    for i in range(N_BRANCH):
        gate = jax.nn.sigmoid(_dot(h, wg_ref[:, i * D_MODEL:(i + 1) * D_MODEL]))
        per = _dot(y_ref[:, i * BRANCH_WIDTH:(i + 1) * BRANCH_WIDTH], wbr_ref[i])
        acc = gate * per if acc is None else acc + gate * per
    o_ref[...] = x + _dot(acc.astype(BF16), wout_ref[...])


def _merge_call(x, y, nw, wg, wbr, wout, tm):
    t = x.shape[0]
    return pl.pallas_call(
        _merge_body,
        out_shape=jax.ShapeDtypeStruct((t, D_MODEL), F32),
        grid=(t // tm,),
        in_specs=[
            pl.BlockSpec((tm, D_MODEL), lambda i: (i, 0)),
            pl.BlockSpec((tm, N_BRANCH * BRANCH_WIDTH), lambda i: (i, 0)),
            _const_spec((1, D_MODEL)),
            _const_spec((D_MODEL, N_BRANCH * D_MODEL)),
            _const_spec((N_BRANCH, BRANCH_WIDTH, D_MODEL)),
            _const_spec((D_MODEL, D_MODEL)),
        ],
        out_specs=pl.BlockSpec((tm, D_MODEL), lambda i: (i, 0)),
        compiler_params=pltpu.CompilerParams(dimension_semantics=("arbitrary",), vmem_limit_bytes=VMEM_LIMIT_BYTES),
        name="merge",
    )(x, y, nw, wg, wbr, wout)


def _branch_body(
        x_ref, cos_ref, sin_ref, nw_ref, win_ref,
        convw_ref, convb_ref, dtb_ref, alog_ref, dskip_ref, ssdn_ref,
        vn_ref, ws_ref, bs_ref,
        qn_ref, wqb_ref, kvn_ref, wkvb_ref, qkq_ref, qkk_ref,
        scw_ref,
        y_ref,
        k_scr, v_scr, st_scr, xbc_stage, sc_stage, z_s, xbc_s, dt_s, u_s, v_s, q_s,
        *, tq):
    j = pl.program_id(1)
    row0 = pl.multiple_of(j * tq, tq)
    q_len = CHUNK

    @pl.when(j == 0)
    def _():
        st_scr[...] = jnp.zeros(st_scr.shape, F32)
        xbc_stage[0:CARRY_ROWS, :] = jnp.zeros((CARRY_ROWS, SSD_XBC), F32)
        sc_stage[0:CARRY_ROWS, :] = jnp.zeros((CARRY_ROWS, SC_WIDTH), F32)

    h = _rms(x_ref[...], nw_ref[...]).astype(BF16)

    z_s[...] = _dot(h, win_ref[:, C_Z:C_Z + SSD_D_INNER])
    xbc_stage[CARRY_ROWS:CARRY_ROWS + tq, :] = _dot(h, win_ref[:, C_XBC:C_XBC + SSD_XBC])
    conv = convb_ref[...]
    for k in range(SSD_CONV):
        off = CARRY_ROWS - (SSD_CONV - 1) + k
        conv = conv + convw_ref[k:k + 1, :] * xbc_stage[off:off + tq, :]
    xbc_stage[0:CARRY_ROWS, :] = xbc_stage[tq:tq + CARRY_ROWS, :]
    xbc_s[...] = _silu(conv)
    dt_raw = _dot(h, win_ref[:, C_DT:C_DT + LANES]) + dtb_ref[...]
    dt_s[...] = jnp.maximum(dt_raw, 0.0) + jnp.log1p(jnp.exp(-jnp.abs(dt_raw)))

    uv = _dot(h, win_ref[:, C_UV:C_UV + 2 * GMLP_WIDTH])
    uv = 0.5 * uv * (1.0 + lax.erf(uv * (1.0 / math.sqrt(2.0))))
    u_s[...] = uv[:, :GMLP_WIDTH]
    v_s[...] = _rms(uv[:, GMLP_WIDTH:], vn_ref[...]).astype(BF16)

    rix = lax.broadcasted_iota(jnp.int32, (q_len, q_len), 0)
    cix = lax.broadcasted_iota(jnp.int32, (q_len, q_len), 1)
    causal = rix >= cix
    tril_bf = jnp.where(causal, 1.0, 0.0).astype(BF16)
    e_row = lax.broadcasted_iota(jnp.int32, (LANES, SSD_D_INNER), 0)
    e_col = lax.broadcasted_iota(jnp.int32, (LANES, SSD_D_INNER), 1)
    expand_bf = jnp.where(lax.shift_right_logical(e_col, 6) == e_row, 1.0, 0.0).astype(BF16)
    head_of_lane = lax.shift_right_logical(lax.broadcasted_iota(jnp.int32, (q_len, SSD_GROUP_WIDTH), 1), 6)
    a_row = -jnp.exp(alog_ref[...])
    ws_masked = [jnp.where(causal, ws_ref[g], 0.0).astype(BF16) for g in range(GMLP_GROUPS)]

    def expand(c):
        hi, lo = _split_bf16(c)
        return _dot(hi, expand_bf) + _dot(lo, expand_bf)

    def chunk_step(c, carry):
        r = pl.multiple_of(c * q_len, q_len)
        rows = pl.ds(r, q_len)
        dt = dt_s[rows, :]
        a = dt * a_row
        a_hi, a_lo = _split_bf16(a)
        acs = _dot(tril_bf, a_hi) + _dot(tril_bf, a_lo)
        acs_t = acs.T
        dt_e = expand(dt)
        acs_e = expand(acs)
        atot_e = acs_e[q_len - 1:q_len, :]
        xs = xbc_s[rows, 0:SSD_D_INNER]
        xdt = xs * dt_e
        xdt_bf = xdt.astype(BF16)
        xw_bf = (xdt * jnp.exp(atot_e - acs_e)).astype(BF16)
        e_acs = jnp.exp(acs_e)
        e_tot = jnp.exp(atot_e)
        ys = []
        for g in range(SSD_GROUPS):
            gl = slice(g * SSD_GROUP_WIDTH, (g + 1) * SSD_GROUP_WIDTH)
            b_g = xbc_s[rows, SSD_D_INNER + g * SSD_STATE:SSD_D_INNER + (g + 1) * SSD_STATE].astype(BF16)
            c_off = SSD_D_INNER + SSD_GROUPS * SSD_STATE
            c_g = xbc_s[rows, c_off + g * SSD_STATE:c_off + (g + 1) * SSD_STATE].astype(BF16)
            cb = _dot_nt(c_g, b_g)
            x_g = xdt_bf[:, gl]
            y_g = None
            for rr in range(SSD_HEADS_PER_GROUP):
                hd = g * SSD_HEADS_PER_GROUP + rr
                seg = acs[:, hd:hd + 1] - acs_t[hd:hd + 1, :]
                m = (cb * jnp.exp(jnp.where(causal, seg, -jnp.inf))).astype(BF16)
                x_h = jnp.where(head_of_lane == rr, x_g, jnp.zeros_like(x_g))
                part = _dot(m, x_h)
                y_g = part if y_g is None else y_g + part
            s_prev = st_scr[g]
            y_g = y_g + _dot(c_g, s_prev.astype(BF16)) * e_acs[:, gl]
            st_scr[g] = e_tot[:, gl] * s_prev + _dot_tn(b_g, xw_bf[:, gl])
            ys.append(y_g)
        y = jnp.concatenate(ys, axis=1) + dskip_ref[...] * xs
        y = y * _silu(z_s[rows, :])
        for g in range(SSD_GROUPS):
            gl = slice(g * SSD_GROUP_WIDTH, (g + 1) * SSD_GROUP_WIDTH)
            y_ref[rows, gl] = _rms(y[:, gl], ssdn_ref[:, gl]).astype(BF16)
        v = v_s[rows, :]
        u = u_s[rows, :]
        for g in range(GMLP_GROUPS):
            gl = slice(g * GMLP_GROUP_WIDTH, (g + 1) * GMLP_GROUP_WIDTH)
            sv = _dot(ws_masked[g], v[:, gl]) + bs_ref[:, gl]
            y_ref[rows, BRANCH_WIDTH + g * GMLP_GROUP_WIDTH:BRANCH_WIDTH + (g + 1) * GMLP_GROUP_WIDTH] = (
                u[:, gl] * sv).astype(BF16)
        return carry

    lax.fori_loop(0, tq // q_len, chunk_step, 0)

    cosf = cos_ref[...]
    sinf = sin_ref[...]
    lane = lax.broadcasted_iota(jnp.int32, (tq, LANES), 1)

    def rope(t):
        swapped = jnp.where(lane < MLA_ROPE // 2, pltpu.roll(t, LANES - MLA_ROPE // 2, 1), pltpu.roll(t, MLA_ROPE // 2, 1))
        return t * cosf + swapped * sinf

    q_lat = _dot(h, win_ref[:, C_Q:C_Q + MLA_Q_RANK])
    q_full = _dot(_rms(q_lat, qn_ref[...]).astype(BF16), wqb_ref[...])
    kv_lat = _dot(h, win_ref[:, C_KV:C_KV + MLA_KV_RANK])
    kv_full = _dot(_rms(kv_lat, kvn_ref[...]).astype(BF16), wkvb_ref[...])
    k_pe = _dot(h, win_ref[:, C_KPE:C_KPE + LANES])
    kpe_ss = jnp.sum(k_pe * k_pe, axis=-1, keepdims=True)
    k_rope = rope(k_pe * qkk_ref[:, LANES:2 * LANES])
    scale = MLA_QK ** -0.5
    blk_rows = pl.ds(row0, tq)
    for hd in range(MLA_HEADS):
        qh = q_full[:, hd * MLA_QK_PAD:(hd + 1) * MLA_QK_PAD]
        rq = lax.rsqrt(jnp.sum(qh * qh, axis=-1, keepdims=True) * (1.0 / MLA_QK) + NORM_EPS) * scale
        q_s[hd, :, 0:LANES] = (qh[:, :LANES] * rq * qkq_ref[:, 0:LANES]).astype(BF16)
        q_s[hd, :, LANES:2 * LANES] = rope(qh[:, LANES:] * rq * qkq_ref[:, LANES:2 * LANES]).astype(BF16)
        kn = kv_full[:, hd * 2 * LANES:hd * 2 * LANES + LANES]
        rk = lax.rsqrt((jnp.sum(kn * kn, axis=-1, keepdims=True) + kpe_ss) * (1.0 / MLA_QK) + NORM_EPS)
        k_scr[hd, blk_rows, 0:LANES] = (kn * rk * qkk_ref[:, 0:LANES]).astype(BF16)
        k_scr[hd, blk_rows, LANES:2 * LANES] = (k_rope * rk).astype(BF16)
        v_scr[hd, blk_rows, :] = kv_full[:, hd * 2 * LANES + LANES:(hd + 1) * 2 * LANES].astype(BF16)

    diag_mask = (lax.broadcasted_iota(jnp.int32, (tq, tq), 0) >= lax.broadcasted_iota(jnp.int32, (tq, tq), 1))
    for hd in range(MLA_HEADS):
        qh = q_s[hd]

        def flash_update(carry, s, v_blk):
            m_i, l_i, acc = carry
            m_new = jnp.maximum(m_i, jnp.max(s, axis=-1, keepdims=True))
            alpha = jnp.exp(m_i - m_new)
            p = jnp.exp(s - m_new)
            l_new = alpha * l_i + jnp.sum(p, axis=-1, keepdims=True)
            acc_new = alpha * acc + _dot(p.astype(BF16), v_blk)
            return m_new, l_new, acc_new

        def kv_step(jb, carry):
            kr = pl.multiple_of(jb * tq, tq)
            s = _dot_nt(qh, k_scr[hd, pl.ds(kr, tq), :])
            return flash_update(carry, s, v_scr[hd, pl.ds(kr, tq), :])

        init = (jnp.full((tq, 1), -jnp.inf, F32), jnp.zeros((tq, 1), F32), jnp.zeros((tq, MLA_V), F32))
        carry = lax.fori_loop(0, j, kv_step, init)
        s = jnp.where(diag_mask, _dot_nt(qh, k_scr[hd, blk_rows, :]), -jnp.inf)
        _, l_f, acc_f = flash_update(carry, s, v_scr[hd, blk_rows, :])
        y_ref[:, 2 * BRANCH_WIDTH + hd * MLA_V:2 * BRANCH_WIDTH + (hd + 1) * MLA_V] = (acc_f / l_f).astype(BF16)

    sc = _dot(h, win_ref[:, C_SC:C_SC + 3 * SC_WIDTH])
    sc_stage[CARRY_ROWS:CARRY_ROWS + tq, :] = sc[:, SC_WIDTH:2 * SC_WIDTH] * sc[:, 2 * SC_WIDTH:]
    conv3 = None
    for k in range(SC_KERNEL):
        off = CARRY_ROWS - (SC_KERNEL - 1) + k
        term = scw_ref[k:k + 1, :] * sc_stage[off:off + tq, :]
        conv3 = term if conv3 is None else conv3 + term
    sc_stage[0:CARRY_ROWS, :] = sc_stage[tq:tq + CARRY_ROWS, :]
    y_ref[:, 3 * BRANCH_WIDTH:4 * BRANCH_WIDTH] = (sc[:, :SC_WIDTH] * conv3).astype(BF16)


def _branch_call(x, cosf, sinf, p, tq):
    b, l, _ = x.shape
    seq_block = lambda w: pl.BlockSpec((None, tq, w), lambda bi, ji: (bi, ji, 0))
    weights = [p["mix_norm"], p["w_in_a"], p["ssd_conv_w"], p["ssd_conv_b"], p["ssd_dt_bias"], p["ssd_a_log"],
               p["ssd_d"], p["ssd_norm"], p["gmlp_v_norm"], p["gmlp_w_s"], p["gmlp_b_s"], p["mla_q_norm"],
               p["mla_w_qb"], p["mla_kv_norm"], p["mla_w_kvb"], p["mla_qk_q"], p["mla_qk_k"], p["sc_conv_w"]]
    return pl.pallas_call(
        functools.partial(_branch_body, tq=tq),
        out_shape=jax.ShapeDtypeStruct((b, l, N_BRANCH * BRANCH_WIDTH), BF16),
        grid=(b, l // tq),
        in_specs=[seq_block(D_MODEL), seq_block(LANES), seq_block(LANES)] + [_const_spec(w.shape) for w in weights],
        out_specs=seq_block(N_BRANCH * BRANCH_WIDTH),
        scratch_shapes=[
            pltpu.VMEM((MLA_HEADS, l, MLA_QK_PAD), BF16),
            pltpu.VMEM((MLA_HEADS, l, MLA_V), BF16),
            pltpu.VMEM((SSD_GROUPS, SSD_STATE, SSD_GROUP_WIDTH), F32),
            pltpu.VMEM((tq + CARRY_ROWS, SSD_XBC), F32),
            pltpu.VMEM((tq + CARRY_ROWS, SC_WIDTH), F32),
            pltpu.VMEM((tq, SSD_D_INNER), F32),
            pltpu.VMEM((tq, SSD_XBC), F32),
            pltpu.VMEM((tq, LANES), F32),
            pltpu.VMEM((tq, GMLP_WIDTH), F32),
            pltpu.VMEM((tq, GMLP_WIDTH), BF16),
            pltpu.VMEM((MLA_HEADS, tq, MLA_QK_PAD), BF16),
        ],
        compiler_params=pltpu.CompilerParams(dimension_semantics=("arbitrary", "arbitrary"),
                                             vmem_limit_bytes=VMEM_LIMIT_BYTES),
        name="branches",
    )(x, cosf, sinf, *weights)


def _pad_cols(w, width):
    return jnp.pad(w, ((0, 0), (0, width - w.shape[1])))


def _prep_layer(l, ffn1_norm, ffn1_w_gu, ffn1_w_down, mix_norm, w_in, ssd_conv_w, ssd_conv_b, ssd_dt_bias,
                ssd_a_log, ssd_d, ssd_norm, gmlp_v_norm, gmlp_w_s, gmlp_b_s, mla_q_norm, mla_w_qb, mla_kv_norm,
                mla_w_kvb, mla_qk_q, mla_qk_k, sc_conv_w, w_branch, w_out, ffn2_norm, ffn2_w_gu, ffn2_w_down):
    row = lambda v: v.reshape(1, -1).astype(F32)
    wi = w_in[l]
    o = 0
    segs = {}
    for name, size in (("z", 512), ("xbc", 1024), ("dt", 8), ("uv", 1024), ("q", 384), ("kv", 128), ("kpe", 64),
                       ("sc", 1536), ("gates", 4096)):
        segs[name] = wi[:, o:o + size]
        o += size
    w_in_a = jnp.concatenate(
        [segs["z"], segs["xbc"], _pad_cols(segs["dt"], LANES), segs["uv"], segs["q"], segs["kv"],
         _pad_cols(segs["kpe"], LANES), segs["sc"]], axis=1).astype(BF16)
    wqb = mla_w_qb[l].reshape(MLA_Q_RANK, MLA_HEADS, MLA_QK)
    wqb = jnp.pad(wqb, ((0, 0), (0, 0), (0, MLA_QK_PAD - MLA_QK))).reshape(MLA_Q_RANK, MLA_HEADS * MLA_QK_PAD)
    pad_qk = lambda v: jnp.pad(v, (0, MLA_QK_PAD - MLA_QK)).reshape(1, MLA_QK_PAD).astype(F32)
    bs_full = jnp.repeat(gmlp_b_s[l].T, GMLP_GROUP_WIDTH, axis=1).astype(F32)
    return {
        "ffn1_norm": row(ffn1_norm[l]), "ffn1_w_gu": ffn1_w_gu[l].astype(BF16), "ffn1_w_down": ffn1_w_down[l].astype(BF16),
        "ffn2_norm": row(ffn2_norm[l]), "ffn2_w_gu": ffn2_w_gu[l].astype(BF16), "ffn2_w_down": ffn2_w_down[l].astype(BF16),
        "mix_norm": row(mix_norm[l]), "w_in_a": w_in_a, "w_gates": segs["gates"].astype(BF16),
        "ssd_conv_w": ssd_conv_w[l].astype(F32), "ssd_conv_b": row(ssd_conv_b[l]),
        "ssd_dt_bias": _pad_cols(row(ssd_dt_bias[l]), LANES), "ssd_a_log": _pad_cols(row(ssd_a_log[l]), LANES),
        "ssd_d": row(jnp.repeat(ssd_d[l], SSD_HEADDIM)), "ssd_norm": row(ssd_norm[l]),
        "gmlp_v_norm": row(gmlp_v_norm[l]), "gmlp_w_s": gmlp_w_s[l].astype(F32), "gmlp_b_s": bs_full,
        "mla_q_norm": row(mla_q_norm[l]), "mla_w_qb": wqb.astype(BF16), "mla_kv_norm": row(mla_kv_norm[l]),
        "mla_w_kvb": mla_w_kvb[l].astype(BF16), "mla_qk_q": pad_qk(mla_qk_q[l]), "mla_qk_k": pad_qk(mla_qk_k[l]),
        "sc_conv_w": sc_conv_w[l].astype(F32),
        "w_branch": w_branch[l].astype(BF16), "w_out": w_out[l].astype(BF16),
    }


def _forward(x, positions, weights, depth, tq, ffn_tm, merge_tm):
    b, l, d = x.shape
    cosf, sinf = _rope_tables(positions)
    xf = x.reshape(b * l, d)
    for layer in range(depth):
        p = _prep_layer(layer, *weights)
        xf = _ffn_call(xf, p["ffn1_norm"], p["ffn1_w_gu"], p["ffn1_w_down"], ffn_tm)
        y = _branch_call(xf.reshape(b, l, d), cosf, sinf, p, tq)
        xf = _merge_call(xf, y.reshape(b * l, N_BRANCH * BRANCH_WIDTH), p["mix_norm"], p["w_gates"], p["w_branch"],
                         p["w_out"], merge_tm)
        xf = _ffn_call(xf, p["ffn2_norm"], p["ffn2_w_gu"], p["ffn2_w_down"], ffn_tm)
    return xf.reshape(b, l, d)


def kernel(x, positions, ffn1_norm, ffn1_w_gu, ffn1_w_down, mix_norm, w_in, ssd_conv_w, ssd_conv_b, ssd_dt_bias, ssd_a_log, ssd_d, ssd_norm, gmlp_v_norm, gmlp_w_s, gmlp_b_s, mla_q_norm, mla_w_qb, mla_kv_norm, mla_w_kvb, mla_qk_q, mla_qk_k, sc_conv_w, w_branch, w_out, ffn2_norm, ffn2_w_gu, ffn2_w_down):
    weights = (ffn1_norm, ffn1_w_gu, ffn1_w_down, mix_norm, w_in, ssd_conv_w, ssd_conv_b, ssd_dt_bias, ssd_a_log,
               ssd_d, ssd_norm, gmlp_v_norm, gmlp_w_s, gmlp_b_s, mla_q_norm, mla_w_qb, mla_kv_norm, mla_w_kvb,
               mla_qk_q, mla_qk_k, sc_conv_w, w_branch, w_out, ffn2_norm, ffn2_w_gu, ffn2_w_down)
    return _forward(x, positions, weights, w_in.shape[0], BRANCH_TQ, FFN_TM, MERGE_TM)
```
